```python
import jax, jax.numpy as jnp
from jax import lax
import numpy as np

D_MODEL = 1024
BATCH = 16
SEQ = 4096
DEPTH = 4

MIX_WIDTH = D_MODEL
POOL_CH = D_MODEL // 4
POOL_GROUPS = 4
POOL_GROUP_DIM = POOL_CH // POOL_GROUPS
POOL_WINDOWS = (2, 4, 8, 16)
CONV_CH = D_MODEL // 4
DW_CONV_LEN = 31
MEM_TOKENS = 256
MEM_HEADS = 4
MEM_CH = D_MODEL // 2
MEM_HEAD_DIM = MEM_CH // MEM_HEADS
IN_PROJ_WIDTH = POOL_CH + 2 * CONV_CH + MEM_CH
N_EXPERTS = 64
TOP_K = 8
EXPERT_DIM = D_MODEL // 4
SHARED_DIM = D_MODEL // 4
ROUTED_SCALE = 2.5
MOE_BLOCK = 128
LN_EPS = 1e-5
DEEPNORM_ALPHA = float((2 * DEPTH) ** 0.25)
DEEPNORM_BETA = float((8 * DEPTH) ** -0.25)

kernel_name = "hymba_pool_conformer_memattn_moe_deepnorm"


def layer_norm(x, g, b):
    xf = x.astype(jnp.float32)
    mu = jnp.mean(xf, axis=-1, keepdims=True)
    var = jnp.mean(jnp.square(xf - mu), axis=-1, keepdims=True)
    y = (xf - mu) * lax.rsqrt(var + LN_EPS) * g.astype(jnp.float32) + b.astype(jnp.float32)
    return y.astype(x.dtype)


def pool_mixer(u, w_pool, pool_scale):
    bsz, seq, _ = u.shape
    uf = u.astype(jnp.float32)
    csum = jnp.concatenate([jnp.zeros_like(uf[:, :1]), jnp.cumsum(uf, axis=1)], axis=1)
    t = jnp.arange(seq)
    pooled = []
    for g, w in enumerate(POOL_WINDOWS):
        lo = w // 2
        hi = w - 1 - lo
        start = jnp.clip(t - lo, 0, seq)
        end = jnp.clip(t + hi + 1, 0, seq)
        cs = csum[:, :, g * POOL_GROUP_DIM:(g + 1) * POOL_GROUP_DIM]
        win_sum = jnp.take(cs, end, axis=1) - jnp.take(cs, start, axis=1)
        cnt = (end - start).astype(jnp.float32)[None, :, None]
        pooled.append(win_sum / cnt)
    pooled = jnp.stack(pooled, axis=2)
    diff = pooled - uf.reshape(bsz, seq, POOL_GROUPS, POOL_GROUP_DIM)
    mixed = jnp.einsum('bsgc,gcd->bsgd', diff, w_pool.astype(jnp.float32))
    out = mixed.reshape(bsz, seq, POOL_CH) * pool_scale.astype(jnp.float32)
    return out.astype(u.dtype)


def conv_mixer(v, w_dw, b_dw, g_cn, b_cn):
    a, gate = jnp.split(v, 2, axis=-1)
    glu = a * jax.nn.sigmoid(gate)
    pad = DW_CONV_LEN // 2
    y = lax.conv_general_dilated(
        glu, w_dw[:, None, :].astype(glu.dtype), window_strides=(1,), padding=[(pad, pad)],
        dimension_numbers=('NWC', 'WIO', 'NWC'), feature_group_count=CONV_CH) + b_dw
    return jax.nn.silu(layer_norm(y, g_cn, b_cn))


def mem_attention(q, mem_n, w_kv):
    bsz, seq, _ = q.shape
    kv = mem_n @ w_kv
    k, v = jnp.split(kv, 2, axis=-1)
    q = q.reshape(bsz, seq, MEM_HEADS, MEM_HEAD_DIM)
    k = k.reshape(bsz, -1, MEM_HEADS, MEM_HEAD_DIM)
    v = v.reshape(bsz, -1, MEM_HEADS, MEM_HEAD_DIM)
    s = jnp.einsum('bshd,bmhd->bhsm', q, k).astype(jnp.float32) * (MEM_HEAD_DIM ** -0.5)
    p = jax.nn.softmax(s, axis=-1).astype(v.dtype)
    o = jnp.einsum('bhsm,bmhd->bshd', p, v)
    return o.reshape(bsz, seq, MEM_CH)


def swiglu(x, wg, wu, wd):
    return (jax.nn.silu(x @ wg) * (x @ wu)) @ wd


def moe(x, w_router, router_bias, w_gate, w_up, w_down, w_gate_sh, w_up_sh, w_down_sh):
    bsz, seq, d = x.shape
    n_tok = bsz * seq
    xt = x.reshape(n_tok, d)
    scores = jax.nn.sigmoid((xt @ w_router).astype(jnp.float32))
    _, idx = lax.top_k(scores + router_bias.astype(jnp.float32), TOP_K)
    sel = jnp.take_along_axis(scores, idx, axis=-1)
    gates = sel / jnp.sum(sel, axis=-1, keepdims=True) * ROUTED_SCALE

    n_assign = n_tok * TOP_K
    e_flat = idx.reshape(-1)
    tok_flat = jnp.arange(n_assign, dtype=jnp.int32) // TOP_K
    gate_flat = gates.reshape(-1)
    order = jnp.argsort(e_flat)
    e_sorted = e_flat[order]
    counts = jnp.bincount(e_flat, length=N_EXPERTS)
    starts = jnp.cumsum(counts) - counts
    padded = (counts + MOE_BLOCK - 1) // MOE_BLOCK * MOE_BLOCK
    padded_end = jnp.cumsum(padded)
    padded_start = padded_end - padded
    dest = padded_start[e_sorted] + jnp.arange(n_assign, dtype=jnp.int32) - starts[e_sorted]
    n_blocks = -(-n_assign // MOE_BLOCK) + N_EXPERTS
    buf_len = n_blocks * MOE_BLOCK
    tok_buf = jnp.full((buf_len,), n_tok, jnp.int32).at[dest].set(tok_flat[order])
    gate_buf = jnp.zeros((buf_len,), jnp.float32).at[dest].set(gate_flat[order])
    block_expert = jnp.minimum(
        jnp.searchsorted(padded_end, jnp.arange(n_blocks) * MOE_BLOCK, side='right'), N_EXPERTS - 1)
    x_pad = jnp.concatenate([xt, jnp.zeros((1, d), xt.dtype)], axis=0)

    def expert_block(acc, blk):
        tok, g, e = blk
        xb = x_pad[tok]
        h = jax.nn.silu(xb @ w_gate[e]) * (xb @ w_up[e])
        y = (h @ w_down[e]) * g[:, None].astype(xb.dtype)
        return acc.at[tok].add(y), None

    routed, _ = lax.scan(
        expert_block, jnp.zeros_like(x_pad),
        (tok_buf.reshape(n_blocks, MOE_BLOCK), gate_buf.reshape(n_blocks, MOE_BLOCK), block_expert))
    shared = swiglu(xt, w_gate_sh, w_up_sh, w_down_sh)
    return (routed[:n_tok] + shared).reshape(bsz, seq, d)


def setup_inputs(seed: int = 0) -> dict:
    key = jax.random.key(seed)
    ks = jax.random.split(key, 26)
    f32 = jnp.float32
    nrm = lambda k, shape, scale: jax.random.normal(k, shape, f32) * scale
    L = DEPTH
    return {
        "x": nrm(ks[0], (BATCH, SEQ, D_MODEL), 1.0),
        "mem": nrm(ks[1], (BATCH, MEM_TOKENS, D_MODEL), 1.0),
        "w_in": nrm(ks[2], (L, D_MODEL, IN_PROJ_WIDTH), D_MODEL ** -0.5),
        "w_pool": nrm(ks[3], (L, POOL_GROUPS, POOL_GROUP_DIM, POOL_GROUP_DIM), POOL_GROUP_DIM ** -0.5),
        "pool_scale": 1.0 + nrm(ks[4], (L, POOL_CH), 0.02),
        "w_dw": nrm(ks[5], (L, DW_CONV_LEN, CONV_CH), DW_CONV_LEN ** -0.5),
        "b_dw": nrm(ks[6], (L, CONV_CH), 0.02),
        "conv_norm_g": 1.0 + nrm(ks[7], (L, CONV_CH), 0.02),
        "conv_norm_b": nrm(ks[8], (L, CONV_CH), 0.02),
        "w_kv": nrm(ks[9], (L, D_MODEL, 2 * MEM_CH), D_MODEL ** -0.5),
        "w_out": nrm(ks[10], (L, MIX_WIDTH, D_MODEL), MIX_WIDTH ** -0.5 * DEEPNORM_BETA),
        "ln1_g": 1.0 + nrm(ks[11], (L, D_MODEL), 0.02),
        "ln1_b": nrm(ks[12], (L, D_MODEL), 0.02),
        "w_router": nrm(ks[13], (L, D_MODEL, N_EXPERTS), D_MODEL ** -0.5),
        "router_bias": nrm(ks[14], (L, N_EXPERTS), 0.01),
        "w_gate": nrm(ks[15], (L, N_EXPERTS, D_MODEL, EXPERT_DIM), D_MODEL ** -0.5),
        "w_up": nrm(ks[16], (L, N_EXPERTS, D_MODEL, EXPERT_DIM), D_MODEL ** -0.5),
        "w_down": nrm(ks[17], (L, N_EXPERTS, EXPERT_DIM, D_MODEL), EXPERT_DIM ** -0.5 * DEEPNORM_BETA),
        "w_gate_sh": nrm(ks[18], (L, D_MODEL, SHARED_DIM), D_MODEL ** -0.5),
        "w_up_sh": nrm(ks[19], (L, D_MODEL, SHARED_DIM), D_MODEL ** -0.5),
        "w_down_sh": nrm(ks[20], (L, SHARED_DIM, D_MODEL), SHARED_DIM ** -0.5 * DEEPNORM_BETA),
        "ln2_g": 1.0 + nrm(ks[21], (L, D_MODEL), 0.02),
        "ln2_b": nrm(ks[22], (L, D_MODEL), 0.02),
        "mem_norm_g": 1.0 + nrm(ks[23], (D_MODEL,), 0.02),
        "mem_norm_b": nrm(ks[24], (D_MODEL,), 0.02),
    }


def reference(x, mem, w_in, w_pool, pool_scale, w_dw, b_dw, conv_norm_g, conv_norm_b, w_kv, w_out,
              ln1_g, ln1_b, w_router, router_bias, w_gate, w_up, w_down, w_gate_sh, w_up_sh, w_down_sh,
              ln2_g, ln2_b, mem_norm_g, mem_norm_b):
    mem_n = layer_norm(mem, mem_norm_g, mem_norm_b)
    for l in range(DEPTH):
        proj = x @ w_in[l]
        u = proj[..., :POOL_CH]
        v = proj[..., POOL_CH:POOL_CH + 2 * CONV_CH]
        q = proj[..., POOL_CH + 2 * CONV_CH:]
        y_pool = pool_mixer(u, w_pool[l], pool_scale[l])
        y_conv = conv_mixer(v, w_dw[l], b_dw[l], conv_norm_g[l], conv_norm_b[l])
        y_mem = mem_attention(q, mem_n, w_kv[l])
        mix = jnp.concatenate([y_pool, y_conv, y_mem], axis=-1) @ w_out[l]
        x = layer_norm(DEEPNORM_ALPHA * x + mix, ln1_g[l], ln1_b[l])
        ffn = moe(x, w_router[l], router_bias[l], w_gate[l], w_up[l], w_down[l],
                  w_gate_sh[l], w_up_sh[l], w_down_sh[l])
        x = layer_norm(DEEPNORM_ALPHA * x + ffn, ln2_g[l], ln2_b[l])
    return x
```

```python
import functools

import jax
import jax.numpy as jnp
from jax import lax
from jax.experimental import pallas as pl
from jax.experimental.pallas import tpu as pltpu

F32 = jnp.float32
BF16 = jnp.bfloat16

POOL_WINDOWS = (2, 4, 8, 16)
MEM_HEADS = 4
TOP_K = 8
ROUTED_SCALE = 2.5
LN_EPS = 1e-5

LANES = 128
HALO = 16
SEQ_TILE = 512
MOE_TILE = 1024
VMEM_LIMIT = 52 * 1024 * 1024


def _layer_norm(z, g, b):
    mu = jnp.mean(z, axis=-1, keepdims=True)
    zc = z - mu
    var = jnp.mean(zc * zc, axis=-1, keepdims=True)
    return zc * lax.rsqrt(var + LN_EPS) * g + b


def _silu(z):
    return z * jax.nn.sigmoid(z)


def _kv_kernel(mem_ref, g_ref, b_ref, wkv_ref, kv_ref):
    mem_n = _layer_norm(mem_ref[0], g_ref[...], b_ref[...])
    kv = jnp.dot(mem_n.astype(BF16), wkv_ref[0], preferred_element_type=F32)
    kv_ref[0, 0] = kv.astype(BF16)


def _kv_call(mem, g, b, w_kv_b):
    n_b, n_m, d = mem.shape
    depth, _, kv_w = w_kv_b.shape
    return pl.pallas_call(
        _kv_kernel,
        grid=(depth, n_b),
        in_specs=[
            pl.BlockSpec((1, n_m, d), lambda l, i: (i, 0, 0)),
            pl.BlockSpec((1, d), lambda l, i: (0, 0)),
            pl.BlockSpec((1, d), lambda l, i: (0, 0)),
            pl.BlockSpec((1, d, kv_w), lambda l, i: (l, 0, 0)),
        ],
        out_specs=pl.BlockSpec((1, 1, n_m, kv_w), lambda l, i: (l, i, 0, 0)),
        out_shape=jax.ShapeDtypeStruct((depth, n_b, n_m, kv_w), BF16),
        compiler_params=pltpu.CompilerParams(dimension_semantics=("parallel", "parallel")),
        name="kv_proj",
    )(mem, g, b, w_kv_b)


def _mix_kernel(xp_ref, xm_ref, xn_ref, w_in_ref, wpool_ref, pscale_ref, wdw_ref, bdw_ref, cng_ref, cnb_ref,
                kv_ref, w_out_ref, l1g_ref, l1b_ref, wr_ref, rb_ref,
                x1_ref, x1b_ref, gates_ref,
                xh_s, mix_s,
                *, seq_len, alpha, pool_ch, conv_ch, n_experts):
    ts = xm_ref.shape[1]
    rows = ts + 2 * HALO
    s = pl.program_id(1)
    n_s = pl.num_programs(1)

    xm = xm_ref[0]
    xh_s[0:HALO] = jnp.where(s > 0, xp_ref[0], 0.0).astype(BF16)
    xh_s[HALO:HALO + ts] = xm.astype(BF16)
    xh_s[HALO + ts:rows] = jnp.where(s < n_s - 1, xn_ref[0], 0.0).astype(BF16)

    uv_w = pool_ch + 2 * conv_ch
    proj_uv = jnp.dot(xh_s[...], w_in_ref[0, :, 0:uv_w], preferred_element_type=F32)
    q = jnp.dot(xh_s[HALO:HALO + ts], w_in_ref[0, :, uv_w:], preferred_element_type=F32)

    u = proj_uv[:, 0:pool_ch]
    s2 = u + pltpu.roll(u, 1, 0)
    s4 = pltpu.roll(s2, 1, 0) + pltpu.roll(s2, rows - 1, 0)
    s8 = pltpu.roll(s4, 2, 0) + pltpu.roll(s4, rows - 2, 0)
    s16 = pltpu.roll(s8, 4, 0) + pltpu.roll(s8, rows - 4, 0)
    group_dim = pool_ch // len(POOL_WINDOWS)
    lane_group = lax.broadcasted_iota(jnp.int32, (1, pool_ch), 1) // group_dim
    t_pos = s * ts + lax.broadcasted_iota(jnp.int32, (ts, 1), 0)
    win_sum = jnp.zeros((ts, pool_ch), F32)
    cnt = jnp.ones((ts, pool_ch), jnp.int32)
    for g, (w, sw) in enumerate(zip(POOL_WINDOWS, (s2, s4, s8, s16))):
        lo = w // 2
        hi = w - 1 - lo
        cnt_w = jnp.minimum(t_pos + hi + 1, seq_len) - jnp.maximum(t_pos - lo, 0)
        in_g = lane_group == g
        win_sum = jnp.where(in_g, sw[HALO:HALO + ts], win_sum)
        cnt = jnp.where(in_g, cnt_w, cnt)
    diff = win_sum / cnt.astype(F32) - u[HALO:HALO + ts]
    y_pool = jnp.dot(diff.astype(BF16), wpool_ref[0], preferred_element_type=F32) * pscale_ref[0]
    mix_s[:, 0:pool_ch] = y_pool.astype(BF16)

    glu = proj_uv[:, pool_ch:pool_ch + conv_ch] * jax.nn.sigmoid(proj_uv[:, pool_ch + conv_ch:uv_w])
    n_taps = wdw_ref.shape[1]
    first = HALO - n_taps // 2
    acc = jnp.zeros((ts, conv_ch), F32) + bdw_ref[0]
    for r in range(8):
        shifted = glu if r == 0 else pltpu.roll(glu, rows - r, 0)
        for k in range(n_taps):
            off = first + k
            if off % 8 == r:
                base = off - r
                acc = acc + shifted[base:base + ts] * wdw_ref[0, k:k + 1, :]
    y_conv = _silu(_layer_norm(acc, cng_ref[0], cnb_ref[0]))
    mix_s[:, pool_ch:pool_ch + conv_ch] = y_conv.astype(BF16)

    mem_ch = q.shape[1]
    hd = mem_ch // MEM_HEADS
    mix_off = pool_ch + conv_ch
    for h in range(MEM_HEADS):
        qh = (q[:, h * hd:(h + 1) * hd] * (hd ** -0.5)).astype(BF16)
        kh = kv_ref[0, 0, :, h * hd:(h + 1) * hd]
        vh = kv_ref[0, 0, :, mem_ch + h * hd:mem_ch + (h + 1) * hd]
        sc = lax.dot_general(qh, kh, (((1,), (1,)), ((), ())), preferred_element_type=F32)
        p = jnp.exp(sc - jnp.max(sc, axis=-1, keepdims=True))
        o = jnp.dot(p.astype(BF16), vh, preferred_element_type=F32) / jnp.sum(p, axis=-1, keepdims=True)
        mix_s[:, mix_off + h * hd:mix_off + (h + 1) * hd] = o.astype(BF16)

    mix = jnp.dot(mix_s[...], w_out_ref[0], preferred_element_type=F32)
    x1 = _layer_norm(alpha * xm + mix, l1g_ref[0], l1b_ref[0])
    x1b = x1.astype(BF16)
    x1_ref[0] = x1
    x1b_ref[0] = x1b

    logits = lax.dot_general(wr_ref[0], x1b, (((1,), (1,)), ((), ())), preferred_element_type=F32)
    scores = jax.nn.sigmoid(logits)
    row = lax.broadcasted_iota(jnp.int32, (n_experts, ts), 0)
    cur = scores + rb_ref[0]
    sel = jnp.zeros((n_experts, ts), jnp.bool_)
    for _ in range(TOP_K):
        m = jnp.max(cur, axis=0, keepdims=True)
        idx = jnp.min(jnp.where(cur == m, row, n_experts), axis=0, keepdims=True)
        hit = row == idx
        sel = jnp.logical_or(sel, hit)
        cur = jnp.where(hit, -jnp.inf, cur)
    picked = jnp.where(sel, scores, 0.0)
    gates_t = picked / jnp.sum(picked, axis=0, keepdims=True) * ROUTED_SCALE
    gates_t = jnp.concatenate([gates_t, jnp.zeros((LANES - n_experts, ts), F32)], axis=0)
    gates_ref[0] = gates_t.T


def _mix_call(layer, x, kv, p, *, alpha):
    n_b, seq_len, d = x.shape
    ts = min(SEQ_TILE, seq_len)
    n_s = seq_len // ts
    halo_blocks = ts // HALO
    n_halo = seq_len // HALO
    pool_ch = p["wpool"].shape[-1]
    conv_ch = p["w_dw"].shape[-1]
    n_experts = p["w_r"].shape[1]
    in_w = p["w_in"].shape[-1]
    n_m, kv_w = kv.shape[2], kv.shape[3]
    n_taps = p["w_dw"].shape[1]

    def lspec(shape):
        return pl.BlockSpec((1,) + shape, lambda i, s: (layer,) + (0,) * len(shape))

    kern = functools.partial(_mix_kernel, seq_len=seq_len, alpha=alpha, pool_ch=pool_ch, conv_ch=conv_ch,
                             n_experts=n_experts)
    return pl.pallas_call(
        kern,
        grid=(n_b, n_s),
        in_specs=[
            pl.BlockSpec((1, HALO, d), lambda i, s: (i, jnp.maximum(s * halo_blocks - 1, 0), 0)),
            pl.BlockSpec((1, ts, d), lambda i, s: (i, s, 0)),
            pl.BlockSpec((1, HALO, d), lambda i, s: (i, jnp.minimum((s + 1) * halo_blocks, n_halo - 1), 0)),
            lspec((d, in_w)),
            lspec((pool_ch, pool_ch)),
            lspec((1, pool_ch)),
            lspec((n_taps, conv_ch)),
            lspec((1, conv_ch)),
            lspec((1, conv_ch)),
            lspec((1, conv_ch)),
            pl.BlockSpec((1, 1, n_m, kv_w), lambda i, s: (layer, i, 0, 0)),
            lspec((d, d)),
            lspec((1, d)),
            lspec((1, d)),
            lspec((n_experts, d)),
            lspec((n_experts, 1)),
        ],
        out_specs=[
            pl.BlockSpec((1, ts, d), lambda i, s: (i, s, 0)),
            pl.BlockSpec((1, ts, d), lambda i, s: (i, s, 0)),
            pl.BlockSpec((1, ts, LANES), lambda i, s: (i, s, 0)),
        ],
        out_shape=[
            jax.ShapeDtypeStruct((n_b, seq_len, d), F32),
            jax.ShapeDtypeStruct((n_b, seq_len, d), BF16),
            jax.ShapeDtypeStruct((n_b, seq_len, LANES), F32),
        ],
        scratch_shapes=[
            pltpu.VMEM((ts + 2 * HALO, d), BF16),
            pltpu.VMEM((ts, d), BF16),
        ],
        compiler_params=pltpu.CompilerParams(dimension_semantics=("parallel", "parallel"),
                                             vmem_limit_bytes=VMEM_LIMIT),
        name="mix_router",
    )(x, x, x, p["w_in"], p["wpool"], p["pool_scale"], p["w_dw"], p["b_dw"], p["cn_g"], p["cn_b"],
      kv, p["w_out"], p["ln1_g"], p["ln1_b"], p["w_r"], p["r_bias"])


def _moe_kernel(x1_ref, xb_ref, g_ref, wgu_ref, wd_ref, wgus_ref, wds_ref, l2g_ref, l2b_ref, o_ref, acc_ref,
                *, alpha):
    e = pl.program_id(1)
    xb = xb_ref[...]
    f_sh = wds_ref.shape[1]
    f_ex = wd_ref.shape[2]

    @pl.when(e == 0)
    def _():
        hs = jnp.dot(xb, wgus_ref[0], preferred_element_type=F32)
        h = _silu(hs[:, 0:f_sh]) * hs[:, f_sh:]
        acc_ref[...] = jnp.dot(h.astype(BF16), wds_ref[0], preferred_element_type=F32)

    lane = lax.broadcasted_iota(jnp.int32, g_ref.shape, 1)
    g = jnp.sum(jnp.where(lane == e, g_ref[...], 0.0), axis=1, keepdims=True)
    hg = jnp.dot(xb, wgu_ref[0, 0], preferred_element_type=F32)
    h = _silu(hg[:, 0:f_ex]) * hg[:, f_ex:] * g
    acc_ref[...] += jnp.dot(h.astype(BF16), wd_ref[0, 0], preferred_element_type=F32)

    @pl.when(e == pl.num_programs(1) - 1)
    def _():
        o_ref[...] = _layer_norm(alpha * x1_ref[...] + acc_ref[...], l2g_ref[0], l2b_ref[0])


def _moe_call(layer, x1, x1b, gates, p, *, alpha):
    n_tok, d = x1.shape
    tm = min(MOE_TILE, n_tok)
    n_experts, _, gu_w = p["w_gu"].shape[1:]
    f_ex = gu_w // 2
    sh_w = p["w_gu_sh"].shape[-1]
    kern = functools.partial(_moe_kernel, alpha=alpha)
    return pl.pallas_call(
        kern,
        grid=(n_tok // tm, n_experts),
        in_specs=[
            pl.BlockSpec((tm, d), lambda i, e: (i, 0)),
            pl.BlockSpec((tm, d), lambda i, e: (i, 0)),
            pl.BlockSpec((tm, LANES), lambda i, e: (i, 0)),
            pl.BlockSpec((1, 1, d, gu_w), lambda i, e: (layer, e, 0, 0)),
            pl.BlockSpec((1, 1, f_ex, d), lambda i, e: (layer, e, 0, 0)),
            pl.BlockSpec((1, d, sh_w), lambda i, e: (layer, 0, 0)),
            pl.BlockSpec((1, sh_w // 2, d), lambda i, e: (layer, 0, 0)),
            pl.BlockSpec((1, 1, d), lambda i, e: (layer, 0, 0)),
            pl.BlockSpec((1, 1, d), lambda i, e: (layer, 0, 0)),
        ],
        out_specs=pl.BlockSpec((tm, d), lambda i, e: (i, 0)),
        out_shape=jax.ShapeDtypeStruct((n_tok, d), F32),
        scratch_shapes=[pltpu.VMEM((tm, d), F32)],
        compiler_params=pltpu.CompilerParams(dimension_semantics=("parallel", "arbitrary"),
                                             vmem_limit_bytes=VMEM_LIMIT),
        name="moe_dense",
    )(x1, x1b, gates, p["w_gu"], p["w_d"], p["w_gu_sh"], p["w_d_sh"], p["ln2_g"], p["ln2_b"])


def kernel(x, mem, w_in, w_pool, pool_scale, w_dw, b_dw, conv_norm_g, conv_norm_b, w_kv, w_out, ln1_g, ln1_b,
           w_router, router_bias, w_gate, w_up, w_down, w_gate_sh, w_up_sh, w_down_sh, ln2_g, ln2_b,
           mem_norm_g, mem_norm_b):
    n_b, seq_len, d = x.shape
    depth = w_in.shape[0]
    alpha = float((2 * depth) ** 0.25)
    n_groups = w_pool.shape[1]

    wpool_bd = jnp.einsum("lgcd,gh->lgchd", w_pool, jnp.eye(n_groups, dtype=w_pool.dtype))
    pool_ch = n_groups * w_pool.shape[2]
    row3 = lambda a: a[:, None, :]
    p = {
        "w_in": w_in.astype(BF16),
        "wpool": wpool_bd.reshape(depth, pool_ch, pool_ch).astype(BF16),
        "pool_scale": row3(pool_scale),
        "w_dw": w_dw,
        "b_dw": row3(b_dw),
        "cn_g": row3(conv_norm_g),
        "cn_b": row3(conv_norm_b),
        "w_out": w_out.astype(BF16),
        "ln1_g": row3(ln1_g),
        "ln1_b": row3(ln1_b),
        "w_r": jnp.swapaxes(w_router, 1, 2).astype(BF16),
        "r_bias": router_bias[:, :, None],
        "w_gu": jnp.concatenate([w_gate, w_up], axis=-1).astype(BF16),
        "w_d": w_down.astype(BF16),
        "w_gu_sh": jnp.concatenate([w_gate_sh, w_up_sh], axis=-1).astype(BF16),
        "w_d_sh": w_down_sh.astype(BF16),
        "ln2_g": row3(ln2_g),
        "ln2_b": row3(ln2_b),
    }
    kv = _kv_call(mem, mem_norm_g[None, :], mem_norm_b[None, :], w_kv.astype(BF16))

    for layer in range(depth):
        x1, x1b, gates = _mix_call(layer, x, kv, p, alpha=alpha)
        n_tok = n_b * seq_len
        x = _moe_call(layer, x1.reshape(n_tok, d), x1b.reshape(n_tok, d), gates.reshape(n_tok, LANES), p,
                      alpha=alpha).reshape(n_b, seq_len, d)
    return x
```

```python
import functools

import jax
import jax.numpy as jnp
from jax import lax
from jax.experimental import pallas as pl
from jax.experimental.pallas import tpu as pltpu

F32 = jnp.float32
BF16 = jnp.bfloat16
I32 = jnp.int32

POOL_WINDOWS = (2, 4, 8, 16)
MEM_HEADS = 4
TOP_K = 8
ROUTED_SCALE = 2.5
LN_EPS = 1e-5

LANES = 128
SUBLANES = 8
HALO = 16
SEQ_TILE = 512
EXPERT_BLOCK = 512
COMBINE_TILE = 256
DMA_UNROLL = 8
VMEM_LIMIT = 52 * 1024 * 1024


def _layer_norm(z, g, b):
    mu = jnp.mean(z, axis=-1, keepdims=True)
    zc = z - mu
    var = jnp.mean(zc * zc, axis=-1, keepdims=True)
    return zc * lax.rsqrt(var + LN_EPS) * g + b


def _silu(z):
    return z * jax.nn.sigmoid(z)


def _store_token_tiles(ref, row0, z):
    n, d = z.shape
    chunks = d // LANES
    for c in range(chunks):
        ref[pl.ds(row0 + c, n, stride=chunks), :] = z[:, c * LANES:(c + 1) * LANES]


def _load_token_tiles(ref, row0, n, d):
    chunks = d // LANES
    return jnp.concatenate([ref[pl.ds(row0 + c, n, stride=chunks), :] for c in range(chunks)], axis=1)


def _kv_kernel(mem_ref, g_ref, b_ref, wkv_ref, kv_ref):
    mem_n = _layer_norm(mem_ref[0], g_ref[...], b_ref[...])
    kv = jnp.dot(mem_n.astype(BF16), wkv_ref[0], preferred_element_type=F32)
    kv_ref[0, 0] = kv.astype(BF16)


def _kv_call(mem, g, b, w_kv_b):
    n_b, n_m, d = mem.shape
    depth, _, kv_w = w_kv_b.shape
    return pl.pallas_call(
        _kv_kernel,
        grid=(depth, n_b),
        in_specs=[
            pl.BlockSpec((1, n_m, d), lambda l, i: (i, 0, 0)),
            pl.BlockSpec((1, d), lambda l, i: (0, 0)),
            pl.BlockSpec((1, d), lambda l, i: (0, 0)),
            pl.BlockSpec((1, d, kv_w), lambda l, i: (l, 0, 0)),
        ],
        out_specs=pl.BlockSpec((1, 1, n_m, kv_w), lambda l, i: (l, i, 0, 0)),
        out_shape=jax.ShapeDtypeStruct((depth, n_b, n_m, kv_w), BF16),
        compiler_params=pltpu.CompilerParams(dimension_semantics=("parallel", "parallel")),
        name="kv_proj",
    )(mem, g, b, w_kv_b)


def _mix_kernel(xp_ref, xm_ref, xn_ref, w_in_ref, wpool_ref, pscale_ref, wdw_ref, bdw_ref, cng_ref, cnb_ref,
                kv_ref, w_out_ref, l1g_ref, l1b_ref, wr_ref, rb_ref,
                x1_ref, x1t_ref, gates_ref, ids_ref, rank_ref, counts_ref,
                xh_s, mix_s, run_s,
                *, seq_len, alpha, pool_ch, conv_ch, n_experts):
    ts = xm_ref.shape[1]
    rows = ts + 2 * HALO
    s = pl.program_id(1)
    n_s = pl.num_programs(1)

    xm = xm_ref[0]
    xh_s[0:HALO] = jnp.where(s > 0, xp_ref[0], 0.0).astype(BF16)
    xh_s[HALO:HALO + ts] = xm.astype(BF16)
    xh_s[HALO + ts:rows] = jnp.where(s < n_s - 1, xn_ref[0], 0.0).astype(BF16)

    uv_w = pool_ch + 2 * conv_ch
    proj_uv = jnp.dot(xh_s[...], w_in_ref[0, :, 0:uv_w], preferred_element_type=F32)
    q = jnp.dot(xh_s[HALO:HALO + ts], w_in_ref[0, :, uv_w:], preferred_element_type=F32)

    u = proj_uv[:, 0:pool_ch]
    s2 = u + pltpu.roll(u, 1, 0)
    s4 = pltpu.roll(s2, 1, 0) + pltpu.roll(s2, rows - 1, 0)
    s8 = pltpu.roll(s4, 2, 0) + pltpu.roll(s4, rows - 2, 0)
    s16 = pltpu.roll(s8, 4, 0) + pltpu.roll(s8, rows - 4, 0)
    group_dim = pool_ch // len(POOL_WINDOWS)
    lane_group = lax.broadcasted_iota(I32, (1, pool_ch), 1) // group_dim
    t_pos = s * ts + lax.broadcasted_iota(I32, (ts, 1), 0)
    win_sum = jnp.zeros((ts, pool_ch), F32)
    cnt = jnp.ones((ts, pool_ch), I32)
    for g, (w, sw) in enumerate(zip(POOL_WINDOWS, (s2, s4, s8, s16))):
        lo = w // 2
        hi = w - 1 - lo
        cnt_w = jnp.minimum(t_pos + hi + 1, seq_len) - jnp.maximum(t_pos - lo, 0)
        in_g = lane_group == g
        win_sum = jnp.where(in_g, sw[HALO:HALO + ts], win_sum)
        cnt = jnp.where(in_g, cnt_w, cnt)
    diff = win_sum / cnt.astype(F32) - u[HALO:HALO + ts]
    y_pool = jnp.dot(diff.astype(BF16), wpool_ref[0], preferred_element_type=F32) * pscale_ref[0]
    mix_s[:, 0:pool_ch] = y_pool.astype(BF16)

    glu = proj_uv[:, pool_ch:pool_ch + conv_ch] * jax.nn.sigmoid(proj_uv[:, pool_ch + conv_ch:uv_w])
    n_taps = wdw_ref.shape[1]
    first = HALO - n_taps // 2
    acc = jnp.zeros((ts, conv_ch), F32) + bdw_ref[0]
    for r in range(8):
        shifted = glu if r == 0 else pltpu.roll(glu, rows - r, 0)
        for k in range(n_taps):
            off = first + k
            if off % 8 == r:
                base = off - r
                acc = acc + shifted[base:base + ts] * wdw_ref[0, k:k + 1, :]
    y_conv = _silu(_layer_norm(acc, cng_ref[0], cnb_ref[0]))
    mix_s[:, pool_ch:pool_ch + conv_ch] = y_conv.astype(BF16)

    mem_ch = q.shape[1]
    hd = mem_ch // MEM_HEADS
    mix_off = pool_ch + conv_ch
    for h in range(MEM_HEADS):
        qh = (q[:, h * hd:(h + 1) * hd] * (hd ** -0.5)).astype(BF16)
        kh = kv_ref[0, 0, :, h * hd:(h + 1) * hd]
        vh = kv_ref[0, 0, :, mem_ch + h * hd:mem_ch + (h + 1) * hd]
        sc = lax.dot_general(qh, kh, (((1,), (1,)), ((), ())), preferred_element_type=F32)
        p = jnp.exp(sc - jnp.max(sc, axis=-1, keepdims=True))
        o = jnp.dot(p.astype(BF16), vh, preferred_element_type=F32) / jnp.sum(p, axis=-1, keepdims=True)
        mix_s[:, mix_off + h * hd:mix_off + (h + 1) * hd] = o.astype(BF16)

    mix = jnp.dot(mix_s[...], w_out_ref[0], preferred_element_type=F32)
    x1 = _layer_norm(alpha * xm + mix, l1g_ref[0], l1b_ref[0])
    x1_ref[0] = x1
    _store_token_tiles(x1t_ref, 0, x1)

    logits = lax.dot_general(wr_ref[0], x1.astype(BF16), (((1,), (1,)), ((), ())),
                             preferred_element_type=F32)
    scores = jax.nn.sigmoid(logits)
    row = lax.broadcasted_iota(I32, (n_experts, ts), 0)
    cur = scores + rb_ref[0]
    sel = jnp.zeros((n_experts, ts), jnp.bool_)
    hits, picked = [], []
    for _ in range(TOP_K):
        m = jnp.max(cur, axis=0, keepdims=True)
        idx = jnp.min(jnp.where(cur == m, row, n_experts), axis=0, keepdims=True)
        hit = row == idx
        hits.append((idx, hit))
        picked.append(jnp.sum(jnp.where(hit, scores, 0.0), axis=0, keepdims=True))
        sel = jnp.logical_or(sel, hit)
        cur = jnp.where(hit, -jnp.inf, cur)
    picked = jnp.concatenate(picked, axis=0)
    gates_k = picked / jnp.sum(picked, axis=0, keepdims=True) * ROUTED_SCALE
    gates_k = jnp.concatenate([gates_k, jnp.zeros((LANES - TOP_K, ts), F32)], axis=0)
    gates_ref[0] = gates_k.T

    @pl.when(jnp.logical_and(pl.program_id(0) == 0, s == 0))
    def _():
        run_s[...] = jnp.zeros_like(run_s)

    earlier = (lax.broadcasted_iota(I32, (ts, ts), 0) < lax.broadcasted_iota(I32, (ts, ts), 1)).astype(BF16)
    sel_f = sel.astype(F32)
    before = jnp.dot(sel_f.astype(BF16), earlier, preferred_element_type=F32)
    rank_full = before + run_s[:, 0:1]
    ranks = [jnp.sum(jnp.where(hit, rank_full, 0.0), axis=0, keepdims=True) for _, hit in hits]
    ids_ref[0] = jnp.concatenate([idx for idx, _ in hits], axis=0)
    rank_ref[0] = jnp.concatenate(ranks, axis=0).astype(I32)
    run_s[...] = run_s[...] + jnp.sum(sel_f, axis=1, keepdims=True)
    counts_ref[...] = run_s[...]


def _mix_call(layer, x, kv, p, *, alpha):
    n_b, seq_len, d = x.shape
    ts = min(SEQ_TILE, seq_len)
    n_s = seq_len // ts
    halo_blocks = ts // HALO
    n_halo = seq_len // HALO
    pool_ch = p["wpool"].shape[-1]
    conv_ch = p["w_dw"].shape[-1]
    n_experts = p["w_r"].shape[1]
    in_w = p["w_in"].shape[-1]
    n_m, kv_w = kv.shape[2], kv.shape[3]
    n_taps = p["w_dw"].shape[1]

    def lspec(shape):
        return pl.BlockSpec((1,) + shape, lambda i, s: (layer,) + (0,) * len(shape))

    tile = lambda w: pl.BlockSpec((1, ts, w), lambda i, s: (i, s, 0))
    choice = pl.BlockSpec((1, TOP_K, ts), lambda i, s: (i, 0, s))
    kern = functools.partial(_mix_kernel, seq_len=seq_len, alpha=alpha, pool_ch=pool_ch, conv_ch=conv_ch,
                             n_experts=n_experts)
    return pl.pallas_call(
        kern,
        grid=(n_b, n_s),
        in_specs=[
            pl.BlockSpec((1, HALO, d), lambda i, s: (i, jnp.maximum(s * halo_blocks - 1, 0), 0)),
            tile(d),
            pl.BlockSpec((1, HALO, d), lambda i, s: (i, jnp.minimum((s + 1) * halo_blocks, n_halo - 1), 0)),
            lspec((d, in_w)),
            lspec((pool_ch, pool_ch)),
            lspec((1, pool_ch)),
            lspec((n_taps, conv_ch)),
            lspec((1, conv_ch)),
            lspec((1, conv_ch)),
            lspec((1, conv_ch)),
            pl.BlockSpec((1, 1, n_m, kv_w), lambda i, s: (layer, i, 0, 0)),
            lspec((d, d)),
            lspec((1, d)),
            lspec((1, d)),
            lspec((n_experts, d)),
            lspec((n_experts, 1)),
        ],
        out_specs=[tile(d), pl.BlockSpec((ts * SUBLANES, LANES), lambda i, s: (i * n_s + s, 0)), tile(LANES),
                   choice, choice, pl.BlockSpec((n_experts, LANES), lambda i, s: (0, 0))],
        out_shape=[
            jax.ShapeDtypeStruct((n_b, seq_len, d), F32),
            jax.ShapeDtypeStruct((n_b * seq_len * SUBLANES, LANES), F32),
            jax.ShapeDtypeStruct((n_b, seq_len, LANES), F32),
            jax.ShapeDtypeStruct((n_b, TOP_K, seq_len), I32),
            jax.ShapeDtypeStruct((n_b, TOP_K, seq_len), I32),
            jax.ShapeDtypeStruct((n_experts, LANES), F32),
        ],
        scratch_shapes=[
            pltpu.VMEM((ts + 2 * HALO, d), BF16),
            pltpu.VMEM((ts, d), BF16),
            pltpu.VMEM((n_experts, LANES), F32),
        ],
        compiler_params=pltpu.CompilerParams(dimension_semantics=("arbitrary", "arbitrary"),
                                             vmem_limit_bytes=VMEM_LIMIT),
        name="mix_router",
    )(x, x, x, p["w_in"], p["wpool"], p["pool_scale"], p["w_dw"], p["b_dw"], p["cn_g"], p["cn_b"],
      kv, p["w_out"], p["ln1_g"], p["ln1_b"], p["w_r"], p["r_bias"])


def _plan_kernel(counts_ref, ids_ref, rank_ref, pos_ref, bexp_ref, fill_ref, *, block, n_rows):
    n_experts = counts_ref.shape[0]
    counts = counts_ref[...]
    padded = jnp.ceil(counts / block) * block
    e_row = lax.broadcasted_iota(I32, counts.shape, 0)
    end = padded
    shift = 1
    while shift < n_experts:
        end = end + jnp.where(e_row >= shift, pltpu.roll(end, shift, 0), 0.0)
        shift *= 2
    start = end - padded
    ids = ids_ref[...]
    pos = rank_ref[...]
    for e in range(n_experts):
        pos = jnp.where(ids == e, pos + start[e:e + 1, 0:1].astype(I32), pos)
    pos_ref[...] = pos
    b_row = lax.broadcasted_iota(I32, (1, bexp_ref.shape[1]), 1).astype(F32) * block
    owner = jnp.sum((end[:, 0:1] <= b_row).astype(I32), axis=0, keepdims=True)
    bexp_ref[...] = jnp.minimum(owner, n_experts - 1)
    fill_hi = jnp.where(e_row == n_experts - 1, float(n_rows), end)
    lane = lax.broadcasted_iota(I32, counts.shape, 1)
    fill_ref[...] = jnp.where(lane == 0, start + counts, fill_hi).astype(I32)


def _plan_call(counts, ids, rank, *, block, n_blocks):
    nb_pad = -(-n_blocks // LANES) * LANES
    pos, bexp, fill = pl.pallas_call(
        functools.partial(_plan_kernel, block=block, n_rows=n_blocks * block),
        out_shape=[jax.ShapeDtypeStruct(ids.shape, I32), jax.ShapeDtypeStruct((1, nb_pad), I32),
                   jax.ShapeDtypeStruct(counts.shape, I32)],
        compiler_params=pltpu.CompilerParams(vmem_limit_bytes=VMEM_LIMIT),
        name="route_plan",
    )(counts, ids, rank)
    return pos, bexp[0, :n_blocks], fill[:, 0:2].T.reshape(-1)


def _tile_copy(src, src_tile, dst, dst_tile, sem):
    return pltpu.make_async_copy(src.at[pl.ds(src_tile * SUBLANES, SUBLANES)],
                                 dst.at[pl.ds(dst_tile * SUBLANES, SUBLANES)], sem)


def _token_rows(pos_ref, t):
    return [pos_ref[0, k, t] for k in range(TOP_K)]


def _wait_tiles(ref, n_tiles, sem):
    rows = n_tiles * SUBLANES
    pltpu.make_async_copy(ref.at[pl.ds(0, rows)], ref.at[pl.ds(0, rows)], sem).wait()


def _dispatch_kernel(pos_ref, fill_ref, x1t_hbm, zero_hbm, xs_hbm, sem, *, block):
    ts = pos_ref.shape[2]
    n_experts = fill_ref.shape[0] // 2
    step = pl.program_id(0) * pl.num_programs(1) + pl.program_id(1)
    tok0 = step * ts

    @pl.when(step == 0)
    def _():
        def per_expert(e, carry):
            def per_row(r, c):
                _tile_copy(zero_hbm, 0, xs_hbm, r, sem).start()
                return c
            return lax.fori_loop(fill_ref[e], fill_ref[n_experts + e], per_row, carry)

        lax.fori_loop(0, n_experts, per_expert, 0)
        for _ in range(n_experts):
            _wait_tiles(xs_hbm, block, sem)

    def issue(j, carry):
        rows = _token_rows(pos_ref, j * DMA_UNROLL)
        for u in range(DMA_UNROLL):
            t = j * DMA_UNROLL + u
            nxt = _token_rows(pos_ref, t + 1) if u + 1 < DMA_UNROLL else None
            for k in range(TOP_K):
                _tile_copy(x1t_hbm, tok0 + t, xs_hbm, rows[k], sem).start(priority=k % 2)
            rows = nxt
        return carry

    lax.fori_loop(0, ts // DMA_UNROLL, issue, 0)
    _wait_tiles(xs_hbm, ts * TOP_K, sem)


def _dispatch_call(pos, fill, x1t, n_rows, *, block):
    n_b, _, seq_len = pos.shape
    ts = min(SEQ_TILE, seq_len)
    return pl.pallas_call(
        functools.partial(_dispatch_kernel, block=block),
        grid=(n_b, seq_len // ts),
        in_specs=[
            pl.BlockSpec((1, TOP_K, ts), lambda i, s: (i, 0, s), memory_space=pltpu.SMEM),
            pl.BlockSpec(memory_space=pltpu.SMEM),
            pl.BlockSpec(memory_space=pl.ANY),
            pl.BlockSpec(memory_space=pl.ANY),
        ],
        out_specs=pl.BlockSpec(memory_space=pl.ANY),
        out_shape=jax.ShapeDtypeStruct((n_rows * SUBLANES, LANES), F32),
        scratch_shapes=[pltpu.SemaphoreType.DMA],
        compiler_params=pltpu.CompilerParams(dimension_semantics=("arbitrary", "arbitrary")),
        name="dispatch_rows",
    )(pos, fill, x1t, jnp.zeros((SUBLANES, LANES), F32))


def _experts_kernel(bexp_ref, xs_ref, wgu_ref, wd_ref, y_ref):
    del bexp_ref
    d, gu_w = wgu_ref.shape[2], wgu_ref.shape[3]
    f_ex = gu_w // 2
    rows = xs_ref.shape[0] // SUBLANES
    xb = _load_token_tiles(xs_ref, 0, rows, d).astype(BF16)
    hg = jnp.dot(xb, wgu_ref[0, 0], preferred_element_type=F32)
    h = _silu(hg[:, 0:f_ex]) * hg[:, f_ex:]
    _store_token_tiles(y_ref, 0, jnp.dot(h.astype(BF16), wd_ref[0, 0], preferred_element_type=F32))


def _experts_call(layer, bexp, xs, p, *, block):
    n_rows = xs.shape[0] // SUBLANES
    d, gu_w = p["w_gu"].shape[-2:]
    f_ex = gu_w // 2
    grid_spec = pltpu.PrefetchScalarGridSpec(
        num_scalar_prefetch=1,
        grid=(n_rows // block,),
        in_specs=[
            pl.BlockSpec((block * SUBLANES, LANES), lambda b, be: (b, 0)),
            pl.BlockSpec((1, 1, d, gu_w), lambda b, be: (layer, be[b], 0, 0)),
            pl.BlockSpec((1, 1, f_ex, d), lambda b, be: (layer, be[b], 0, 0)),
        ],
        out_specs=pl.BlockSpec((block * SUBLANES, LANES), lambda b, be: (b, 0)),
    )
    return pl.pallas_call(
        _experts_kernel,
        grid_spec=grid_spec,
        out_shape=jax.ShapeDtypeStruct(xs.shape, F32),
        compiler_params=pltpu.CompilerParams(dimension_semantics=("parallel",), vmem_limit_bytes=VMEM_LIMIT),
        name="experts_grouped",
    )(bexp, xs, p["w_gu"], p["w_d"])


def _combine_kernel(pos_ref, y_hbm, x1_ref, g_ref, wgus_ref, wds_ref, l2g_ref, l2b_ref, o_ref, ybuf, sem,
                    *, alpha):
    tm, d = x1_ref.shape

    def issue(j, carry):
        rows = _token_rows(pos_ref, j * DMA_UNROLL)
        for u in range(DMA_UNROLL):
            t = j * DMA_UNROLL + u
            nxt = _token_rows(pos_ref, t + 1) if u + 1 < DMA_UNROLL else None
            for k in range(TOP_K):
                _tile_copy(y_hbm, rows[k], ybuf, k * tm + t, sem).start(priority=k % 2)
            rows = nxt
        return carry

    lax.fori_loop(0, tm // DMA_UNROLL, issue, 0)

    x1 = x1_ref[...]
    f_sh = wds_ref.shape[1]
    hs = jnp.dot(x1.astype(BF16), wgus_ref[0], preferred_element_type=F32)
    h = _silu(hs[:, 0:f_sh]) * hs[:, f_sh:]
    ffn = jnp.dot(h.astype(BF16), wds_ref[0], preferred_element_type=F32)

    _wait_tiles(ybuf, tm * TOP_K, sem)
    g = g_ref[...]
    for k in range(TOP_K):
        ffn = ffn + _load_token_tiles(ybuf, k * tm * SUBLANES, tm, d) * g[:, k:k + 1]
    o_ref[...] = _layer_norm(alpha * x1 + ffn, l2g_ref[0], l2b_ref[0])


def _combine_call(layer, pos, y, x1, gates, p, *, alpha):
    n_tok, d = x1.shape
    n_b, _, seq_len = pos.shape
    tm = min(COMBINE_TILE, seq_len)
    per_seq = seq_len // tm
    sh_w = p["w_gu_sh"].shape[-1]
    tile = lambda w: pl.BlockSpec((tm, w), lambda i: (i, 0))
    return pl.pallas_call(
        functools.partial(_combine_kernel, alpha=alpha),
        grid=(n_tok // tm,),
        in_specs=[
            pl.BlockSpec((1, TOP_K, tm), lambda i: (i // per_seq, 0, i % per_seq), memory_space=pltpu.SMEM),
            pl.BlockSpec(memory_space=pl.ANY),
            tile(d),
            tile(LANES),
            pl.BlockSpec((1, d, sh_w), lambda i: (layer, 0, 0)),
            pl.BlockSpec((1, sh_w // 2, d), lambda i: (layer, 0, 0)),
            pl.BlockSpec((1, 1, d), lambda i: (layer, 0, 0)),
            pl.BlockSpec((1, 1, d), lambda i: (layer, 0, 0)),
        ],
        out_specs=tile(d),
        out_shape=jax.ShapeDtypeStruct((n_tok, d), F32),
        scratch_shapes=[pltpu.VMEM((TOP_K * tm * SUBLANES, LANES), F32), pltpu.SemaphoreType.DMA],
        compiler_params=pltpu.CompilerParams(dimension_semantics=("arbitrary",), vmem_limit_bytes=VMEM_LIMIT),
        name="combine_shared_ln",
    )(pos, y, x1, gates, p["w_gu_sh"], p["w_d_sh"], p["ln2_g"], p["ln2_b"])


def kernel(x, mem, w_in, w_pool, pool_scale, w_dw, b_dw, conv_norm_g, conv_norm_b, w_kv, w_out, ln1_g, ln1_b,
           w_router, router_bias, w_gate, w_up, w_down, w_gate_sh, w_up_sh, w_down_sh, ln2_g, ln2_b,
           mem_norm_g, mem_norm_b):
    n_b, seq_len, d = x.shape
    assert d == SUBLANES * LANES, "token-tile layout: one (8, 128) f32 tile per token row"
    depth = w_in.shape[0]
    alpha = float((2 * depth) ** 0.25)
    n_groups = w_pool.shape[1]
    n_experts = w_router.shape[-1]
    n_tok = n_b * seq_len
    block = EXPERT_BLOCK
    n_blocks = -(-(n_tok * TOP_K) // block) + n_experts
    n_rows = n_blocks * block

    wpool_bd = jnp.einsum("lgcd,gh->lgchd", w_pool, jnp.eye(n_groups, dtype=w_pool.dtype))
    pool_ch = n_groups * w_pool.shape[2]
    row3 = lambda a: a[:, None, :]
    p = {
        "w_in": w_in.astype(BF16),
        "wpool": wpool_bd.reshape(depth, pool_ch, pool_ch).astype(BF16),
        "pool_scale": row3(pool_scale),
        "w_dw": w_dw,
        "b_dw": row3(b_dw),
        "cn_g": row3(conv_norm_g),
        "cn_b": row3(conv_norm_b),
        "w_out": w_out.astype(BF16),
        "ln1_g": row3(ln1_g),
        "ln1_b": row3(ln1_b),
        "w_r": jnp.swapaxes(w_router, 1, 2).astype(BF16),
        "r_bias": router_bias[:, :, None],
        "w_gu": jnp.concatenate([w_gate, w_up], axis=-1).astype(BF16),
        "w_d": w_down.astype(BF16),
        "w_gu_sh": jnp.concatenate([w_gate_sh, w_up_sh], axis=-1).astype(BF16),
        "w_d_sh": w_down_sh.astype(BF16),
        "ln2_g": row3(ln2_g),
        "ln2_b": row3(ln2_b),
    }
    kv = _kv_call(mem, mem_norm_g[None, :], mem_norm_b[None, :], w_kv.astype(BF16))

    for layer in range(depth):
        x1, x1t, gates, ids, rank, counts = _mix_call(layer, x, kv, p, alpha=alpha)
        pos, bexp, fill = _plan_call(counts, ids.reshape(n_b * TOP_K, seq_len),
                                     rank.reshape(n_b * TOP_K, seq_len), block=block, n_blocks=n_blocks)
        pos = pos.reshape(n_b, TOP_K, seq_len)
        xs = _dispatch_call(pos, fill, x1t, n_rows, block=block)
        y = _experts_call(layer, bexp, xs, p, block=block)
        x = _combine_call(layer, pos, y, x1.reshape(n_tok, d), gates.reshape(n_tok, LANES), p,
                          alpha=alpha).reshape(n_b, seq_len, d)
    return x
```

```python
import functools

import jax
import jax.numpy as jnp
from jax import lax
from jax.experimental import pallas as pl
from jax.experimental.pallas import tpu as pltpu

F32 = jnp.float32
BF16 = jnp.bfloat16
I32 = jnp.int32
U32 = jnp.uint32

POOL_WINDOWS = (2, 4, 8, 16)
MEM_HEADS = 4
TOP_K = 8
ROUTED_SCALE = 2.5
LN_EPS = 1e-5

LANES = 128
SUBLANES = 8
HALO = 16
SEQ_TILE = 512
EXPERT_BLOCK = 512
COMBINE_TILE = 256
DMA_UNROLL = 8
VMEM_LIMIT = 52 * 1024 * 1024


def _layer_norm(z, g, b):
    mu = jnp.mean(z, axis=-1, keepdims=True)
    zc = z - mu
    var = jnp.mean(zc * zc, axis=-1, keepdims=True)
    return zc * lax.rsqrt(var + LN_EPS) * g + b


def _silu(z):
    return z * jax.nn.sigmoid(z)


def _pack_bf16_pairs(z):
    half = z.shape[1] // 2
    lo = lax.bitcast_convert_type(z[:, :half].astype(BF16).astype(F32), U32)
    hi = lax.bitcast_convert_type(z[:, half:].astype(BF16).astype(F32), U32)
    return hi | (lo >> 16)


def _unpack_bf16_pairs(w):
    lo = lax.bitcast_convert_type(w << 16, F32).astype(BF16)
    hi = lax.bitcast_convert_type(w & jnp.uint32(0xFFFF0000), F32).astype(BF16)
    return lo, hi


def _store_token_tiles(ref, row0, z):
    n, d = z.shape
    chunks = d // LANES
    for c in range(chunks):
        ref[pl.ds(row0 + c, n, stride=chunks), :] = z[:, c * LANES:(c + 1) * LANES]


def _load_token_tiles(ref, row0, n, d):
    chunks = d // LANES
    return jnp.concatenate([ref[pl.ds(row0 + c, n, stride=chunks), :] for c in range(chunks)], axis=1)


def _kv_kernel(mem_ref, g_ref, b_ref, wkv_ref, kv_ref):
    mem_n = _layer_norm(mem_ref[0], g_ref[...], b_ref[...])
    kv = jnp.dot(mem_n.astype(BF16), wkv_ref[0], preferred_element_type=F32)
    kv_ref[0, 0] = kv.astype(BF16)


def _kv_call(mem, g, b, w_kv_b):
    n_b, n_m, d = mem.shape
    depth, _, kv_w = w_kv_b.shape
    return pl.pallas_call(
        _kv_kernel,
        grid=(depth, n_b),
        in_specs=[
            pl.BlockSpec((1, n_m, d), lambda l, i: (i, 0, 0)),
            pl.BlockSpec((1, d), lambda l, i: (0, 0)),
            pl.BlockSpec((1, d), lambda l, i: (0, 0)),
            pl.BlockSpec((1, d, kv_w), lambda l, i: (l, 0, 0)),
        ],
        out_specs=pl.BlockSpec((1, 1, n_m, kv_w), lambda l, i: (l, i, 0, 0)),
        out_shape=jax.ShapeDtypeStruct((depth, n_b, n_m, kv_w), BF16),
        compiler_params=pltpu.CompilerParams(dimension_semantics=("parallel", "parallel")),
        name="kv_proj",
    )(mem, g, b, w_kv_b)


def _mix_kernel(xp_ref, xm_ref, xn_ref, w_in_ref, wpool_ref, pscale_ref, wdw_ref, bdw_ref, cng_ref, cnb_ref,
                kv_ref, w_out_ref, l1g_ref, l1b_ref, wr_ref, rb_ref,
                x1_ref, x1t_ref, gates_ref, ids_ref, rank_ref, counts_ref,
                xh_s, mix_s, run_s,
                *, seq_len, alpha, pool_ch, conv_ch, n_experts):
    ts = xm_ref.shape[1]
    rows = ts + 2 * HALO
    s = pl.program_id(1)
    n_s = pl.num_programs(1)

    xm = xm_ref[0]
    xh_s[0:HALO] = jnp.where(s > 0, xp_ref[0], 0.0).astype(BF16)
    xh_s[HALO:HALO + ts] = xm.astype(BF16)
    xh_s[HALO + ts:rows] = jnp.where(s < n_s - 1, xn_ref[0], 0.0).astype(BF16)

    uv_w = pool_ch + 2 * conv_ch
    proj_uv = jnp.dot(xh_s[...], w_in_ref[0, :, 0:uv_w], preferred_element_type=F32)
    q = jnp.dot(xh_s[HALO:HALO + ts], w_in_ref[0, :, uv_w:], preferred_element_type=F32)

    u = proj_uv[:, 0:pool_ch]
    s2 = u + pltpu.roll(u, 1, 0)
    s4 = pltpu.roll(s2, 1, 0) + pltpu.roll(s2, rows - 1, 0)
    s8 = pltpu.roll(s4, 2, 0) + pltpu.roll(s4, rows - 2, 0)
    s16 = pltpu.roll(s8, 4, 0) + pltpu.roll(s8, rows - 4, 0)
    group_dim = pool_ch // len(POOL_WINDOWS)
    lane_group = lax.broadcasted_iota(I32, (1, pool_ch), 1) // group_dim
    t_pos = s * ts + lax.broadcasted_iota(I32, (ts, 1), 0)
    win_sum = jnp.zeros((ts, pool_ch), F32)
    cnt = jnp.ones((ts, pool_ch), I32)
    for g, (w, sw) in enumerate(zip(POOL_WINDOWS, (s2, s4, s8, s16))):
        lo = w // 2
        hi = w - 1 - lo
        cnt_w = jnp.minimum(t_pos + hi + 1, seq_len) - jnp.maximum(t_pos - lo, 0)
        in_g = lane_group == g
        win_sum = jnp.where(in_g, sw[HALO:HALO + ts], win_sum)
        cnt = jnp.where(in_g, cnt_w, cnt)
    diff = win_sum / cnt.astype(F32) - u[HALO:HALO + ts]
    y_pool = jnp.dot(diff.astype(BF16), wpool_ref[0], preferred_element_type=F32) * pscale_ref[0]
    mix_s[:, 0:pool_ch] = y_pool.astype(BF16)

    glu = proj_uv[:, pool_ch:pool_ch + conv_ch] * jax.nn.sigmoid(proj_uv[:, pool_ch + conv_ch:uv_w])
    n_taps = wdw_ref.shape[1]
    first = HALO - n_taps // 2
    acc = jnp.zeros((ts, conv_ch), F32) + bdw_ref[0]
    for r in range(8):
        shifted = glu if r == 0 else pltpu.roll(glu, rows - r, 0)
        for k in range(n_taps):
            off = first + k
            if off % 8 == r:
                base = off - r
                acc = acc + shifted[base:base + ts] * wdw_ref[0, k:k + 1, :]
    y_conv = _silu(_layer_norm(acc, cng_ref[0], cnb_ref[0]))
    mix_s[:, pool_ch:pool_ch + conv_ch] = y_conv.astype(BF16)

    mem_ch = q.shape[1]
    hd = mem_ch // MEM_HEADS
    mix_off = pool_ch + conv_ch
    for h in range(MEM_HEADS):
        qh = (q[:, h * hd:(h + 1) * hd] * (hd ** -0.5)).astype(BF16)
        kh = kv_ref[0, 0, :, h * hd:(h + 1) * hd]
        vh = kv_ref[0, 0, :, mem_ch + h * hd:mem_ch + (h + 1) * hd]
        sc = lax.dot_general(qh, kh, (((1,), (1,)), ((), ())), preferred_element_type=F32)
        p = jnp.exp(sc - jnp.max(sc, axis=-1, keepdims=True))
        o = jnp.dot(p.astype(BF16), vh, preferred_element_type=F32) / jnp.sum(p, axis=-1, keepdims=True)
        mix_s[:, mix_off + h * hd:mix_off + (h + 1) * hd] = o.astype(BF16)

    mix = jnp.dot(mix_s[...], w_out_ref[0], preferred_element_type=F32)
    x1 = _layer_norm(alpha * xm + mix, l1g_ref[0], l1b_ref[0])
    x1_ref[0] = x1
    _store_token_tiles(x1t_ref, 0, _pack_bf16_pairs(x1))

    logits = lax.dot_general(wr_ref[0], x1.astype(BF16), (((1,), (1,)), ((), ())),
                             preferred_element_type=F32)
    scores = jax.nn.sigmoid(logits)
    row = lax.broadcasted_iota(I32, (n_experts, ts), 0)
    cur = scores + rb_ref[0]
    sel = jnp.zeros((n_experts, ts), jnp.bool_)
    hits, picked = [], []
    for _ in range(TOP_K):
        m = jnp.max(cur, axis=0, keepdims=True)
        idx = jnp.min(jnp.where(cur == m, row, n_experts), axis=0, keepdims=True)
        hit = row == idx
        hits.append((idx, hit))
        picked.append(jnp.sum(jnp.where(hit, scores, 0.0), axis=0, keepdims=True))
        sel = jnp.logical_or(sel, hit)
        cur = jnp.where(hit, -jnp.inf, cur)
    picked = jnp.concatenate(picked, axis=0)
    gates_k = picked / jnp.sum(picked, axis=0, keepdims=True) * ROUTED_SCALE
    gates_k = jnp.concatenate([gates_k, jnp.zeros((LANES - TOP_K, ts), F32)], axis=0)
    gates_ref[0] = gates_k.T

    @pl.when(jnp.logical_and(pl.program_id(0) == 0, s == 0))
    def _():
        run_s[...] = jnp.zeros_like(run_s)

    earlier = (lax.broadcasted_iota(I32, (ts, ts), 0) < lax.broadcasted_iota(I32, (ts, ts), 1)).astype(BF16)
    sel_f = sel.astype(F32)
    before = jnp.dot(sel_f.astype(BF16), earlier, preferred_element_type=F32)
    rank_full = before + run_s[:, 0:1]
    ranks = [jnp.sum(jnp.where(hit, rank_full, 0.0), axis=0, keepdims=True) for _, hit in hits]
    ids_ref[0] = jnp.concatenate([idx for idx, _ in hits], axis=0)
    rank_ref[0] = jnp.concatenate(ranks, axis=0).astype(I32)
    run_s[...] = run_s[...] + jnp.sum(sel_f, axis=1, keepdims=True)
    counts_ref[...] = run_s[...]


def _mix_call(layer, x, kv, p, *, alpha):
    n_b, seq_len, d = x.shape
    ts = min(SEQ_TILE, seq_len)
    n_s = seq_len // ts
    halo_blocks = ts // HALO
    n_halo = seq_len // HALO
    pool_ch = p["wpool"].shape[-1]
    conv_ch = p["w_dw"].shape[-1]
    n_experts = p["w_r"].shape[1]
    in_w = p["w_in"].shape[-1]
    n_m, kv_w = kv.shape[2], kv.shape[3]
    n_taps = p["w_dw"].shape[1]
    x_rows = d // 2 // LANES

    def lspec(shape):
        return pl.BlockSpec((1,) + shape, lambda i, s: (layer,) + (0,) * len(shape))

    tile = lambda w: pl.BlockSpec((1, ts, w), lambda i, s: (i, s, 0))
    choice = pl.BlockSpec((1, TOP_K, ts), lambda i, s: (i, 0, s))
    kern = functools.partial(_mix_kernel, seq_len=seq_len, alpha=alpha, pool_ch=pool_ch, conv_ch=conv_ch,
                             n_experts=n_experts)
    return pl.pallas_call(
        kern,
        grid=(n_b, n_s),
        in_specs=[
            pl.BlockSpec((1, HALO, d), lambda i, s: (i, jnp.maximum(s * halo_blocks - 1, 0), 0)),
            tile(d),
            pl.BlockSpec((1, HALO, d), lambda i, s: (i, jnp.minimum((s + 1) * halo_blocks, n_halo - 1), 0)),
            lspec((d, in_w)),
            lspec((pool_ch, pool_ch)),
            lspec((1, pool_ch)),
            lspec((n_taps, conv_ch)),
            lspec((1, conv_ch)),
            lspec((1, conv_ch)),
            lspec((1, conv_ch)),
            pl.BlockSpec((1, 1, n_m, kv_w), lambda i, s: (layer, i, 0, 0)),
            lspec((d, d)),
            lspec((1, d)),
            lspec((1, d)),
            lspec((n_experts, d)),
            lspec((n_experts, 1)),
        ],
        out_specs=[tile(d), pl.BlockSpec((ts * x_rows, LANES), lambda i, s: (i * n_s + s, 0)), tile(LANES),
                   choice, choice, pl.BlockSpec((n_experts, LANES), lambda i, s: (0, 0))],
        out_shape=[
            jax.ShapeDtypeStruct((n_b, seq_len, d), F32),
            jax.ShapeDtypeStruct((n_b * seq_len * x_rows, LANES), U32),
            jax.ShapeDtypeStruct((n_b, seq_len, LANES), F32),
            jax.ShapeDtypeStruct((n_b, TOP_K, seq_len), I32),
            jax.ShapeDtypeStruct((n_b, TOP_K, seq_len), I32),
            jax.ShapeDtypeStruct((n_experts, LANES), F32),
        ],
        scratch_shapes=[
            pltpu.VMEM((ts + 2 * HALO, d), BF16),
            pltpu.VMEM((ts, d), BF16),
            pltpu.VMEM((n_experts, LANES), F32),
        ],
        compiler_params=pltpu.CompilerParams(dimension_semantics=("arbitrary", "arbitrary"),
                                             vmem_limit_bytes=VMEM_LIMIT),
        name="mix_router",
    )(x, x, x, p["w_in"], p["wpool"], p["pool_scale"], p["w_dw"], p["b_dw"], p["cn_g"], p["cn_b"],
      kv, p["w_out"], p["ln1_g"], p["ln1_b"], p["w_r"], p["r_bias"])


def _plan_kernel(counts_ref, ids_ref, rank_ref, pos_ref, bexp_ref, fill_ref, *, block, n_rows):
    n_experts = counts_ref.shape[0]
    counts = counts_ref[...]
    padded = jnp.ceil(counts / block) * block
    e_row = lax.broadcasted_iota(I32, counts.shape, 0)
    end = padded
    shift = 1
    while shift < n_experts:
        end = end + jnp.where(e_row >= shift, pltpu.roll(end, shift, 0), 0.0)
        shift *= 2
    start = end - padded
    ids = ids_ref[...]
    pos = rank_ref[...]
    for e in range(n_experts):
        pos = jnp.where(ids == e, pos + start[e:e + 1, 0:1].astype(I32), pos)
    pos_ref[...] = pos
    b_row = lax.broadcasted_iota(I32, (1, bexp_ref.shape[1]), 1).astype(F32) * block
    owner = jnp.sum((end[:, 0:1] <= b_row).astype(I32), axis=0, keepdims=True)
    bexp_ref[...] = jnp.minimum(owner, n_experts - 1)
    fill_hi = jnp.where(e_row == n_experts - 1, float(n_rows), end)
    lane = lax.broadcasted_iota(I32, counts.shape, 1)
    fill_ref[...] = jnp.where(lane == 0, start + counts, fill_hi).astype(I32)


def _plan_call(counts, ids, rank, *, block, n_blocks):
    nb_pad = -(-n_blocks // LANES) * LANES
    pos, bexp, fill = pl.pallas_call(
        functools.partial(_plan_kernel, block=block, n_rows=n_blocks * block),
        out_shape=[jax.ShapeDtypeStruct(ids.shape, I32), jax.ShapeDtypeStruct((1, nb_pad), I32),
                   jax.ShapeDtypeStruct(counts.shape, I32)],
        compiler_params=pltpu.CompilerParams(vmem_limit_bytes=VMEM_LIMIT),
        name="route_plan",
    )(counts, ids, rank)
    return pos, bexp[0, :n_blocks], fill[:, 0:2].T.reshape(-1)


def _tile_copy(src, src_tile, dst, dst_tile, rows, sem):
    return pltpu.make_async_copy(src.at[pl.ds(src_tile * rows, rows)], dst.at[pl.ds(dst_tile * rows, rows)], sem)


def _token_rows(pos_ref, t):
    return [pos_ref[0, k, t] for k in range(TOP_K)]


def _wait_tiles(ref, n_tiles, rows, sem):
    n = n_tiles * rows
    pltpu.make_async_copy(ref.at[pl.ds(0, n)], ref.at[pl.ds(0, n)], sem).wait()


def _dispatch_kernel(pos_ref, fill_ref, x1t_ref, xs_hbm, zero_s, sem, *, block, x_rows):
    ts = pos_ref.shape[2]
    n_experts = fill_ref.shape[0] // 2
    step = pl.program_id(0) * pl.num_programs(1) + pl.program_id(1)

    @pl.when(step == 0)
    def _():
        zero_s[...] = jnp.zeros_like(zero_s)

        def per_expert(e, carry):
            def per_row(r, c):
                _tile_copy(zero_s, 0, xs_hbm, r, x_rows, sem).start()
                return c
            return lax.fori_loop(fill_ref[e], fill_ref[n_experts + e], per_row, carry)

        lax.fori_loop(0, n_experts, per_expert, 0)
        for _ in range(n_experts):
            _wait_tiles(xs_hbm, block, x_rows, sem)

    def issue(j, carry):
        rows = _token_rows(pos_ref, j * DMA_UNROLL)
        for u in range(DMA_UNROLL):
            t = j * DMA_UNROLL + u
            nxt = _token_rows(pos_ref, t + 1) if u + 1 < DMA_UNROLL else None
            for k in range(TOP_K):
                _tile_copy(x1t_ref, t, xs_hbm, rows[k], x_rows, sem).start(priority=k % 2)
            rows = nxt
        return carry

    lax.fori_loop(0, ts // DMA_UNROLL, issue, 0)
    _wait_tiles(xs_hbm, ts * TOP_K, x_rows, sem)


def _dispatch_call(pos, fill, x1t, n_rows, *, block):
    n_b, _, seq_len = pos.shape
    ts = min(SEQ_TILE, seq_len)
    n_s = seq_len // ts
    x_rows = x1t.shape[0] // (n_b * seq_len)
    return pl.pallas_call(
        functools.partial(_dispatch_kernel, block=block, x_rows=x_rows),
        grid=(n_b, n_s),
        in_specs=[
            pl.BlockSpec((1, TOP_K, ts), lambda i, s: (i, 0, s), memory_space=pltpu.SMEM),
            pl.BlockSpec(memory_space=pltpu.SMEM),
            pl.BlockSpec((ts * x_rows, LANES), lambda i, s: (i * n_s + s, 0)),
        ],
        out_specs=pl.BlockSpec(memory_space=pl.ANY),
        out_shape=jax.ShapeDtypeStruct((n_rows * x_rows, LANES), x1t.dtype),
        scratch_shapes=[pltpu.VMEM((SUBLANES, LANES), x1t.dtype), pltpu.SemaphoreType.DMA],
        compiler_params=pltpu.CompilerParams(dimension_semantics=("arbitrary", "arbitrary")),
        name="dispatch_rows",
    )(pos, fill, x1t)


def _experts_kernel(bexp_ref, xs_ref, wgu_ref, wd_ref, y_ref, *, x_rows):
    del bexp_ref
    d, gu_w = wgu_ref.shape[2], wgu_ref.shape[3]
    f_ex = gu_w // 2
    n = xs_ref.shape[0] // x_rows
    lo, hi = _unpack_bf16_pairs(_load_token_tiles(xs_ref, 0, n, d // 2))
    hg = (jnp.dot(lo, wgu_ref[0, 0, 0:d // 2], preferred_element_type=F32)
          + jnp.dot(hi, wgu_ref[0, 0, d // 2:d], preferred_element_type=F32))
    h = _silu(hg[:, 0:f_ex]) * hg[:, f_ex:]
    _store_token_tiles(y_ref, 0, jnp.dot(h.astype(BF16), wd_ref[0, 0], preferred_element_type=F32))


def _experts_call(layer, bexp, xs, n_rows, p, *, block):
    d, gu_w = p["w_gu"].shape[-2:]
    f_ex = gu_w // 2
    x_rows = xs.shape[0] // n_rows
    y_rows = d // LANES
    grid_spec = pltpu.PrefetchScalarGridSpec(
        num_scalar_prefetch=1,
        grid=(n_rows // block,),
        in_specs=[
            pl.BlockSpec((block * x_rows, LANES), lambda b, be: (b, 0)),
            pl.BlockSpec((1, 1, d, gu_w), lambda b, be: (layer, be[b], 0, 0)),
            pl.BlockSpec((1, 1, f_ex, d), lambda b, be: (layer, be[b], 0, 0)),
        ],
        out_specs=pl.BlockSpec((block * y_rows, LANES), lambda b, be: (b, 0)),
    )
    return pl.pallas_call(
        functools.partial(_experts_kernel, x_rows=x_rows),
        grid_spec=grid_spec,
        out_shape=jax.ShapeDtypeStruct((n_rows * y_rows, LANES), F32),
        compiler_params=pltpu.CompilerParams(dimension_semantics=("parallel",), vmem_limit_bytes=VMEM_LIMIT),
        name="experts_grouped",
    )(bexp, xs, p["w_gu"], p["w_d"])


def _combine_kernel(pos_ref, y_hbm, x1_ref, g_ref, wgus_ref, wds_ref, l2g_ref, l2b_ref, o_ref, ybuf, sem,
                    *, alpha):
    tm, d = x1_ref.shape
    y_rows = d // LANES

    def issue(j, carry):
        rows = _token_rows(pos_ref, j * DMA_UNROLL)
        for u in range(DMA_UNROLL):
            t = j * DMA_UNROLL + u
            nxt = _token_rows(pos_ref, t + 1) if u + 1 < DMA_UNROLL else None
            for k in range(TOP_K):
                _tile_copy(y_hbm, rows[k], ybuf, k * tm + t, y_rows, sem).start(priority=k % 2)
            rows = nxt
        return carry

    lax.fori_loop(0, tm // DMA_UNROLL, issue, 0)

    x1 = x1_ref[...]
    f_sh = wds_ref.shape[1]
    hs = jnp.dot(x1.astype(BF16), wgus_ref[0], preferred_element_type=F32)
    h = _silu(hs[:, 0:f_sh]) * hs[:, f_sh:]
    ffn = jnp.dot(h.astype(BF16), wds_ref[0], preferred_element_type=F32)

    _wait_tiles(ybuf, tm * TOP_K, y_rows, sem)
    g = g_ref[...]
    for k in range(TOP_K):
        ffn = ffn + _load_token_tiles(ybuf, k * tm * y_rows, tm, d) * g[:, k:k + 1]
    o_ref[...] = _layer_norm(alpha * x1 + ffn, l2g_ref[0], l2b_ref[0])


def _combine_call(layer, pos, y, x1, gates, p, *, alpha):
    n_tok, d = x1.shape
    n_b, _, seq_len = pos.shape
    tm = min(COMBINE_TILE, seq_len)
    per_seq = seq_len // tm
    sh_w = p["w_gu_sh"].shape[-1]
    tile = lambda w: pl.BlockSpec((tm, w), lambda i: (i, 0))
    return pl.pallas_call(
        functools.partial(_combine_kernel, alpha=alpha),
        grid=(n_tok // tm,),
        in_specs=[
            pl.BlockSpec((1, TOP_K, tm), lambda i: (i // per_seq, 0, i % per_seq), memory_space=pltpu.SMEM),
            pl.BlockSpec(memory_space=pl.ANY),
            tile(d),
            tile(LANES),
            pl.BlockSpec((1, d, sh_w), lambda i: (layer, 0, 0)),
            pl.BlockSpec((1, sh_w // 2, d), lambda i: (layer, 0, 0)),
            pl.BlockSpec((1, 1, d), lambda i: (layer, 0, 0)),
            pl.BlockSpec((1, 1, d), lambda i: (layer, 0, 0)),
        ],
        out_specs=tile(d),
        out_shape=jax.ShapeDtypeStruct((n_tok, d), F32),
        scratch_shapes=[pltpu.VMEM((TOP_K * tm * (d // LANES), LANES), F32), pltpu.SemaphoreType.DMA],
        compiler_params=pltpu.CompilerParams(dimension_semantics=("arbitrary",), vmem_limit_bytes=VMEM_LIMIT),
        name="combine_shared_ln",
    )(pos, y, x1, gates, p["w_gu_sh"], p["w_d_sh"], p["ln2_g"], p["ln2_b"])


def kernel(x, mem, w_in, w_pool, pool_scale, w_dw, b_dw, conv_norm_g, conv_norm_b, w_kv, w_out, ln1_g, ln1_b,
           w_router, router_bias, w_gate, w_up, w_down, w_gate_sh, w_up_sh, w_down_sh, ln2_g, ln2_b,
           mem_norm_g, mem_norm_b):
    n_b, seq_len, d = x.shape
    assert d == SUBLANES * LANES, "token-tile layout: one (8, 128) f32 tile per token row"
    depth = w_in.shape[0]
    alpha = float((2 * depth) ** 0.25)
    n_groups = w_pool.shape[1]
    n_experts = w_router.shape[-1]
    n_tok = n_b * seq_len
    block = EXPERT_BLOCK
    n_blocks = -(-(n_tok * TOP_K) // block) + n_experts
    n_rows = n_blocks * block

    wpool_bd = jnp.einsum("lgcd,gh->lgchd", w_pool, jnp.eye(n_groups, dtype=w_pool.dtype))
    pool_ch = n_groups * w_pool.shape[2]
    row3 = lambda a: a[:, None, :]
    p = {
        "w_in": w_in.astype(BF16),
        "wpool": wpool_bd.reshape(depth, pool_ch, pool_ch).astype(BF16),
        "pool_scale": row3(pool_scale),
        "w_dw": w_dw,
        "b_dw": row3(b_dw),
        "cn_g": row3(conv_norm_g),
        "cn_b": row3(conv_norm_b),
        "w_out": w_out.astype(BF16),
        "ln1_g": row3(ln1_g),
        "ln1_b": row3(ln1_b),
        "w_r": jnp.swapaxes(w_router, 1, 2).astype(BF16),
        "r_bias": router_bias[:, :, None],
        "w_gu": jnp.concatenate([w_gate, w_up], axis=-1).astype(BF16),
        "w_d": w_down.astype(BF16),
        "w_gu_sh": jnp.concatenate([w_gate_sh, w_up_sh], axis=-1).astype(BF16),
        "w_d_sh": w_down_sh.astype(BF16),
        "ln2_g": row3(ln2_g),
        "ln2_b": row3(ln2_b),
    }
    kv = _kv_call(mem, mem_norm_g[None, :], mem_norm_b[None, :], w_kv.astype(BF16))

    for layer in range(depth):
        x1, x1t, gates, ids, rank, counts = _mix_call(layer, x, kv, p, alpha=alpha)
        pos, bexp, fill = _plan_call(counts, ids.reshape(n_b * TOP_K, seq_len),
                                     rank.reshape(n_b * TOP_K, seq_len), block=block, n_blocks=n_blocks)
        pos = pos.reshape(n_b, TOP_K, seq_len)
        xs = _dispatch_call(pos, fill, x1t, n_rows, block=block)
        y = _experts_call(layer, bexp, xs, n_rows, p, block=block)
        x = _combine_call(layer, pos, y, x1.reshape(n_tok, d), gates.reshape(n_tok, LANES), p,
                          alpha=alpha).reshape(n_b, seq_len, d)
    return x
```
